```python
import jax, jax.numpy as jnp
from jax import lax
import numpy as np

D_MODEL = 1024
BATCH = 4
SEQ = 4096
DEPTH = 2
DEC_BATCH = 32
DEC_SEQ = 8
PAST_LEN = 16384
PAGE_SIZE = 128

N_MIXERS = 2
HEAD_DIM = 64
MIX_W = D_MODEL
MEM_HEADS = 4
MEM_W = MEM_HEADS * HEAD_DIM
N_MEM = 256
FOX_W = MIX_W - MEM_W
FOX_HEADS = FOX_W // HEAD_DIM
CONV_CH = MIX_W - MEM_W
CONV_WIDTH = 31
D_FF = ((8 * D_MODEL + 3 * 256 - 1) // (3 * 256)) * 256
Q_BLOCK = 128
N_CONV_LAYERS = (DEPTH + 1) // 2
N_FOX_LAYERS = DEPTH // 2
FORGET_BIAS_MIN = 3.0
FORGET_BIAS_MAX = 12.0
EPS = 1e-6
NEG_INF = -1e30

kernel_name = 'hybrid_conformer_conv_fox_memory_decoder_step'


def rmsnorm(x, g):
    xf = x.astype(jnp.float32)
    y = xf * lax.rsqrt(jnp.mean(xf * xf, axis=-1, keepdims=True) + EPS)
    return (y * g.astype(jnp.float32)).astype(x.dtype)


def layernorm(x, g, b):
    xf = x.astype(jnp.float32)
    mu = jnp.mean(xf, axis=-1, keepdims=True)
    xc = xf - mu
    y = xc * lax.rsqrt(jnp.mean(xc * xc, axis=-1, keepdims=True) + EPS)
    return (y * g.astype(jnp.float32) + b.astype(jnp.float32)).astype(x.dtype)


def swiglu(h, w_gu, w_dn):
    gu = h @ w_gu
    return (jax.nn.silu(gu[..., :D_FF]) * gu[..., D_FF:]) @ w_dn


def conv_module(z, buf, dw, dw_b, ln_g, ln_b):
    u = z[..., :CONV_CH] * jax.nn.sigmoid(z[..., CONV_CH:])
    u_full = jnp.concatenate([buf.astype(u.dtype), u], axis=1)
    y = lax.conv_general_dilated(u_full, dw[:, None, :].astype(u.dtype), (1,), 'VALID',
                                 dimension_numbers=('NWC', 'WIO', 'NWC'),
                                 feature_group_count=CONV_CH) + dw_b.astype(u.dtype)
    y = jax.nn.silu(layernorm(y, ln_g, ln_b))
    return y, u_full[:, -(CONV_WIDTH - 1):]


def fox_heads(z, b_f, g_q, g_k):
    b, l, _ = z.shape
    shp = (b, l, FOX_HEADS, HEAD_DIM)
    q = rmsnorm(z[..., :FOX_W].reshape(shp), g_q)
    k = rmsnorm(z[..., FOX_W:2 * FOX_W].reshape(shp), g_k)
    v = z[..., 2 * FOX_W:3 * FOX_W].reshape(shp)
    logf = jax.nn.log_sigmoid(z[..., 3 * FOX_W:3 * FOX_W + FOX_HEADS].astype(jnp.float32)
                              + b_f.astype(jnp.float32))
    return q, k, v, logf


def fox_prompt(q, k, v, logf):
    b, s, h, d = q.shape
    nb = s // Q_BLOCK
    scale = HEAD_DIM ** -0.5
    c_k = jnp.transpose(lax.cumsum(logf, axis=1), (0, 2, 1))
    qb = q.reshape(b, nb, Q_BLOCK, h, d).transpose(1, 0, 2, 3, 4)
    cb = c_k.reshape(b, h, nb, Q_BLOCK).transpose(2, 0, 1, 3)
    pos = jnp.arange(s, dtype=jnp.int32)
    pb = pos.reshape(nb, Q_BLOCK)

    def block(args):
        qi, ci, pi = args
        sc = jnp.einsum('bqhd,bkhd->bhqk', qi, k).astype(jnp.float32) * scale
        sc = sc + ci[..., :, None] - c_k[:, :, None, :]
        sc = jnp.where((pi[:, None] >= pos[None, :])[None, None], sc, NEG_INF)
        p = jax.nn.softmax(sc, axis=-1)
        return jnp.einsum('bhqk,bkhd->bqhd', p.astype(v.dtype), v)

    out = lax.map(block, (qb, cb, pb))
    return out.transpose(1, 0, 2, 3, 4).reshape(b, s, h, d)


def fox_sample(q, k, v, logf, k_pool, v_pool, lf_pool, page_table):
    bd, t, h, d = q.shape
    p_len = page_table.shape[1] * k_pool.shape[1]
    scale = HEAD_DIM ** -0.5
    k_past = k_pool[page_table].reshape(bd, p_len, h, d)
    v_past = v_pool[page_table].reshape(bd, p_len, h, d)
    lf_past = lf_pool[page_table].reshape(bd, p_len, h).astype(jnp.float32)
    r_past = lax.cumsum(lf_past, axis=1, reverse=True) - lf_past
    cq = jnp.transpose(lax.cumsum(logf, axis=1), (0, 2, 1))
    s_past = (jnp.einsum('bqhd,bkhd->bhqk', q, k_past).astype(jnp.float32) * scale
              + cq[..., :, None] + jnp.transpose(r_past, (0, 2, 1))[:, :, None, :])
    s_new = (jnp.einsum('bqhd,bkhd->bhqk', q, k).astype(jnp.float32) * scale
             + cq[..., :, None] - cq[:, :, None, :])
    causal = jnp.tril(jnp.ones((t, t), dtype=bool))
    s_new = jnp.where(causal[None, None], s_new, NEG_INF)
    p = jax.nn.softmax(jnp.concatenate([s_past, s_new], axis=-1), axis=-1)
    return (jnp.einsum('bhqk,bkhd->bqhd', p[..., :p_len].astype(v.dtype), v_past)
            + jnp.einsum('bhqk,bkhd->bqhd', p[..., p_len:].astype(v.dtype), v))


def memory_kv(mem, g_norm, w_kv, g_k):
    kv = rmsnorm(mem, g_norm) @ w_kv
    b, n, _ = kv.shape
    k = rmsnorm(kv[..., :MEM_W].reshape(b, n, MEM_HEADS, HEAD_DIM), g_k)
    v = kv[..., MEM_W:].reshape(b, n, MEM_HEADS, HEAD_DIM)
    return k, v


def memory_attend(qm, km, vm):
    sc = jnp.einsum('bqhd,bmhd->bhqm', qm, km).astype(jnp.float32) * (HEAD_DIM ** -0.5)
    p = jax.nn.softmax(sc, axis=-1)
    return jnp.einsum('bhqm,bmhd->bqhd', p.astype(vm.dtype), vm)


def finish_layer(x, mix, qm, km, vm, g_qm, w_o, g2, w_gu, w_dn):
    b, l, _ = x.shape
    qm = rmsnorm(qm.reshape(b, l, MEM_HEADS, HEAD_DIM), g_qm)
    mo = memory_attend(qm, km, vm).reshape(b, l, MEM_W)
    x = x + jnp.concatenate([mix, mo], axis=-1) @ w_o
    return x + swiglu(rmsnorm(x, g2), w_gu, w_dn)


def setup_inputs(seed: int = 0) -> dict:
    key = jax.random.key(seed)
    ks = iter(jax.random.split(key, 40))

    def nrm(shape, scale=1.0):
        return jax.random.normal(next(ks), shape, jnp.float32) * scale

    def gain(shape):
        return 1.0 + nrm(shape, 0.02)

    nc, nf = N_CONV_LAYERS, N_FOX_LAYERS
    n_pages = PAST_LEN // PAGE_SIZE
    n_used = DEC_BATCH * n_pages
    n_phys = n_used + max(1, n_used // 4)
    perm = jax.random.permutation(next(ks), n_phys)
    page_table = perm[:n_used].reshape(DEC_BATCH, n_pages).astype(jnp.int32)
    head_bias = jnp.linspace(FORGET_BIAS_MIN, FORGET_BIAS_MAX, FOX_HEADS, dtype=jnp.float32)
    return {
        'x_prompt': nrm((BATCH, SEQ, D_MODEL)),
        'x_sample': nrm((DEC_BATCH, DEC_SEQ, D_MODEL)),
        'cache_conv': nrm((nc, DEC_BATCH, CONV_WIDTH - 1, CONV_CH), 0.5),
        'cache_fox_k': nrm((nf, n_phys, PAGE_SIZE, FOX_HEADS, HEAD_DIM)),
        'cache_fox_v': nrm((nf, n_phys, PAGE_SIZE, FOX_HEADS, HEAD_DIM)),
        'cache_fox_logf': jax.nn.log_sigmoid(head_bias + nrm((nf, n_phys, PAGE_SIZE, FOX_HEADS))),
        'cache_mem_k': nrm((DEPTH, DEC_BATCH, N_MEM, MEM_HEADS, HEAD_DIM)),
        'cache_mem_v': nrm((DEPTH, DEC_BATCH, N_MEM, MEM_HEADS, HEAD_DIM)),
        'page_table': page_table,
        'mem_prompt': nrm((BATCH, N_MEM, D_MODEL)),
        'norm1': gain((DEPTH, D_MODEL)),
        'conv_w_in': nrm((nc, D_MODEL, 2 * CONV_CH + MEM_W), D_MODEL ** -0.5),
        'conv_dw': nrm((nc, CONV_WIDTH, CONV_CH), CONV_WIDTH ** -0.5),
        'conv_dw_b': nrm((nc, CONV_CH), 0.02),
        'conv_ln_g': gain((nc, CONV_CH)),
        'conv_ln_b': nrm((nc, CONV_CH), 0.02),
        'fox_w_in': nrm((nf, D_MODEL, 3 * FOX_W + FOX_HEADS + MEM_W), D_MODEL ** -0.5),
        'fox_b_f': head_bias + nrm((nf, FOX_HEADS), 0.1),
        'fox_q_norm': gain((nf, HEAD_DIM)),
        'fox_k_norm': gain((nf, HEAD_DIM)),
        'mem_norm': gain((DEPTH, D_MODEL)),
        'w_mem_kv': nrm((DEPTH, D_MODEL, 2 * MEM_W), D_MODEL ** -0.5),
        'mem_q_norm': gain((DEPTH, HEAD_DIM)),
        'mem_k_norm': gain((DEPTH, HEAD_DIM)),
        'w_out': nrm((DEPTH, MIX_W, D_MODEL), MIX_W ** -0.5),
        'norm2': gain((DEPTH, D_MODEL)),
        'w_gate_up': nrm((DEPTH, D_MODEL, 2 * D_FF), D_MODEL ** -0.5),
        'w_down': nrm((DEPTH, D_FF, D_MODEL), D_FF ** -0.5),
    }


def reference(x_prompt, x_sample, cache_conv, cache_fox_k, cache_fox_v, cache_fox_logf,
              cache_mem_k, cache_mem_v, page_table, mem_prompt, norm1, conv_w_in, conv_dw,
              conv_dw_b, conv_ln_g, conv_ln_b, fox_w_in, fox_b_f, fox_q_norm, fox_k_norm,
              mem_norm, w_mem_kv, mem_q_norm, mem_k_norm, w_out, norm2, w_gate_up, w_down):
    xp, xs = x_prompt, x_sample
    conv_p, conv_s = [], []
    fk_p, fv_p, fl_p, fk_s, fv_s, fl_s = [], [], [], [], [], []
    mk_p, mv_p = [], []
    for i in range(DEPTH):
        hp = rmsnorm(xp, norm1[i])
        hs = rmsnorm(xs, norm1[i])
        km_p, vm_p = memory_kv(mem_prompt, mem_norm[i], w_mem_kv[i], mem_k_norm[i])
        mk_p.append(km_p)
        mv_p.append(vm_p)
        j = i // N_MIXERS
        if i % N_MIXERS == 0:
            zp = hp @ conv_w_in[j]
            zs = hs @ conv_w_in[j]
            buf0 = jnp.zeros((xp.shape[0], CONV_WIDTH - 1, CONV_CH), zp.dtype)
            mix_p, st_p = conv_module(zp[..., :2 * CONV_CH], buf0, conv_dw[j], conv_dw_b[j],
                                      conv_ln_g[j], conv_ln_b[j])
            mix_s, st_s = conv_module(zs[..., :2 * CONV_CH], cache_conv[j], conv_dw[j], conv_dw_b[j],
                                      conv_ln_g[j], conv_ln_b[j])
            conv_p.append(st_p)
            conv_s.append(st_s)
            qm_p = zp[..., 2 * CONV_CH:]
            qm_s = zs[..., 2 * CONV_CH:]
        else:
            zp = hp @ fox_w_in[j]
            zs = hs @ fox_w_in[j]
            qp, kp, vp, lfp = fox_heads(zp, fox_b_f[j], fox_q_norm[j], fox_k_norm[j])
            qs, ks_, vs, lfs = fox_heads(zs, fox_b_f[j], fox_q_norm[j], fox_k_norm[j])
            mix_p = fox_prompt(qp, kp, vp, lfp).reshape(xp.shape[0], xp.shape[1], FOX_W)
            mix_s = fox_sample(qs, ks_, vs, lfs, cache_fox_k[j], cache_fox_v[j], cache_fox_logf[j],
                               page_table).reshape(xs.shape[0], xs.shape[1], FOX_W)
            fk_p.append(kp)
            fv_p.append(vp)
            fl_p.append(lfp)
            fk_s.append(ks_)
            fv_s.append(vs)
            fl_s.append(lfs)
            qm_p = zp[..., 3 * FOX_W + FOX_HEADS:]
            qm_s = zs[..., 3 * FOX_W + FOX_HEADS:]
        xp = finish_layer(xp, mix_p, qm_p, km_p, vm_p, mem_q_norm[i], w_out[i], norm2[i],
                          w_gate_up[i], w_down[i])
        xs = finish_layer(xs, mix_s, qm_s, cache_mem_k[i], cache_mem_v[i], mem_q_norm[i], w_out[i],
                          norm2[i], w_gate_up[i], w_down[i])
    return (xp, xs, jnp.stack(conv_p), jnp.stack(fk_p), jnp.stack(fv_p), jnp.stack(fl_p),
            jnp.stack(mk_p), jnp.stack(mv_p), jnp.stack(conv_s), jnp.stack(fk_s), jnp.stack(fv_s),
            jnp.stack(fl_s))
```

```python
import functools

import jax
import jax.numpy as jnp
from jax import lax
from jax.experimental import pallas as pl
from jax.experimental.pallas import tpu as pltpu

F32 = jnp.float32
BF16 = jnp.bfloat16

HEAD_DIM = 64
MEM_HEADS = 4
MEM_W = MEM_HEADS * HEAD_DIM
FOX_HEADS = 12
FOX_W = FOX_HEADS * HEAD_DIM
CONV_CH = 768
CONV_WIDTH = 31
CONV_HIST = 32
LANES = 128
EPS = 1e-6
NEG_INF = -1e30
SCALE = HEAD_DIM ** -0.5
VMEM_LIMIT = 56 * 1024 * 1024


def _params(*sem):
    return pltpu.CompilerParams(dimension_semantics=sem, vmem_limit_bytes=VMEM_LIMIT)


def _rms(x, g):
    return x * lax.rsqrt(jnp.mean(x * x, axis=-1, keepdims=True) + EPS) * g


def _dot(a, b):
    return jnp.dot(a, b, preferred_element_type=F32)


def _dot_nt(a, b):
    return lax.dot_general(a, b, (((1,), (1,)), ((), ())), preferred_element_type=F32)


def _head_ms(x, bd_ref):
    return _dot((x * x).astype(BF16), bd_ref[...]) * (1.0 / HEAD_DIM)


def _split3(x):
    a = x.astype(BF16)
    r = x - a.astype(F32)
    b = r.astype(BF16)
    c = (r - b.astype(F32)).astype(BF16)
    return a, b, c


def _block_diag_ones(n_heads):
    return jnp.kron(jnp.eye(n_heads, dtype=F32), jnp.ones((HEAD_DIM, HEAD_DIM), F32)).astype(BF16)


def _inproj_conv_kernel(x_ref, g_ref, w_ref, u_ref, qm_ref):
    h = _rms(x_ref[...], g_ref[...])
    z = _dot(h.astype(BF16), w_ref[...])
    u_ref[...] = z[:, :CONV_CH] * jax.nn.sigmoid(z[:, CONV_CH:2 * CONV_CH])
    qm_ref[...] = z[:, 2 * CONV_CH:]


def _inproj_conv(x, g, w, tm):
    t, d = x.shape
    n = w.shape[1]
    return pl.pallas_call(
        _inproj_conv_kernel,
        grid=(t // tm,),
        in_specs=[
            pl.BlockSpec((tm, d), lambda i: (i, 0)),
            pl.BlockSpec((1, d), lambda i: (0, 0)),
            pl.BlockSpec((d, n), lambda i: (0, 0)),
        ],
        out_specs=[
            pl.BlockSpec((tm, CONV_CH), lambda i: (i, 0)),
            pl.BlockSpec((tm, MEM_W), lambda i: (i, 0)),
        ],
        out_shape=[jax.ShapeDtypeStruct((t, CONV_CH), F32), jax.ShapeDtypeStruct((t, MEM_W), F32)],
        compiler_params=_params("parallel"),
        name="inproj_conv",
    )(x, g.reshape(1, d), w)


def _conv_kernel(u_ref, hist_ref, dw_ref, b_ref, g_ref, beta_ref, o_ref, sc_ref, *, tt, rc, zero_first):
    hist = hist_ref[0]
    if zero_first:
        hist = jnp.where(pl.program_id(1) == 0, 0.0, hist)
    sc_ref[0:CONV_HIST, :] = hist
    sc_ref[CONV_HIST:CONV_HIST + tt, :] = u_ref[0]
    off = CONV_HIST - (CONV_WIDTH - 1)
    for c in range(tt // rc):
        base = c * rc
        acc = jnp.zeros((rc, CONV_CH), F32)
        for w in range(CONV_WIDTH):
            acc = acc + dw_ref[w:w + 1, :] * sc_ref[base + off + w:base + off + w + rc, :]
        y = acc + b_ref[...]
        mu = jnp.mean(y, axis=-1, keepdims=True)
        yc = y - mu
        yn = yc * lax.rsqrt(jnp.mean(yc * yc, axis=-1, keepdims=True) + EPS) * g_ref[...] + beta_ref[...]
        o_ref[0, base:base + rc, :] = yn * jax.nn.sigmoid(yn)


def _conv(u, hist, dw, b, g, beta, tt, rc):
    bsz, length, ch = u.shape
    zero_first = hist is None
    if zero_first:
        ratio = tt // CONV_HIST
        hist_arr = u
        hist_spec = pl.BlockSpec((1, CONV_HIST, ch), lambda bi, i: (bi, jnp.maximum(i * ratio - 1, 0), 0))
    else:
        hist_arr = hist
        hist_spec = pl.BlockSpec((1, CONV_HIST, ch), lambda bi, i: (bi, 0, 0))
    vec = pl.BlockSpec((1, ch), lambda bi, i: (0, 0))
    return pl.pallas_call(
        functools.partial(_conv_kernel, tt=tt, rc=rc, zero_first=zero_first),
        grid=(bsz, length // tt),
        in_specs=[
            pl.BlockSpec((1, tt, ch), lambda bi, i: (bi, i, 0)),
            hist_spec,
            pl.BlockSpec((CONV_WIDTH, ch), lambda bi, i: (0, 0)),
            vec, vec, vec,
        ],
        out_specs=pl.BlockSpec((1, tt, ch), lambda bi, i: (bi, i, 0)),
        out_shape=jax.ShapeDtypeStruct((bsz, length, ch), F32),
        scratch_shapes=[pltpu.VMEM((CONV_HIST + tt, ch), F32)],
        compiler_params=_params("parallel", "parallel"),
        name="conv_mixer",
    )(u, hist_arr, dw, b.reshape(1, ch), g.reshape(1, ch), beta.reshape(1, ch))


def _memkv_kernel(x_ref, g_ref, w_ref, gk_ref, bd_ref, k_ref, v_ref):
    h = _rms(x_ref[...], g_ref[...])
    kv = _dot(h.astype(BF16), w_ref[...])
    k = kv[:, :MEM_W]
    k_ref[...] = k * lax.rsqrt(_head_ms(k, bd_ref) + EPS) * gk_ref[...]
    v_ref[...] = kv[:, MEM_W:]


def _memkv(mem, g, w, gk, tm):
    t, d = mem.shape
    out = jax.ShapeDtypeStruct((t, MEM_W), F32)
    return pl.pallas_call(
        _memkv_kernel,
        grid=(t // tm,),
        in_specs=[
            pl.BlockSpec((tm, d), lambda i: (i, 0)),
            pl.BlockSpec((1, d), lambda i: (0, 0)),
            pl.BlockSpec((d, 2 * MEM_W), lambda i: (0, 0)),
            pl.BlockSpec((1, MEM_W), lambda i: (0, 0)),
            pl.BlockSpec((MEM_W, MEM_W), lambda i: (0, 0)),
        ],
        out_specs=[pl.BlockSpec((tm, MEM_W), lambda i: (i, 0))] * 2,
        out_shape=[out, out],
        compiler_params=_params("parallel"),
        name="memory_kv",
    )(mem, g.reshape(1, d), w, jnp.tile(gk, MEM_HEADS).reshape(1, MEM_W), _block_diag_ones(MEM_HEADS))


def _out_kernel(x_ref, mix_ref, qm_ref, km_ref, vm_ref, gq_ref, wo_ref, o_ref, mo_ref):
    qm = qm_ref[0]
    mix_w = mix_ref.shape[-1]
    for h in range(MEM_HEADS):
        sl = slice(h * HEAD_DIM, (h + 1) * HEAD_DIM)
        qh = _rms(qm[:, sl], gq_ref[...]) * SCALE
        s = _dot_nt(qh.astype(BF16), km_ref[0, :, sl])
        p = jnp.exp(s - jnp.max(s, axis=-1, keepdims=True))
        o = _dot(p.astype(BF16), vm_ref[0, :, sl])
        mo_ref[:, sl] = o / jnp.sum(p, axis=-1, keepdims=True)
    y = _dot(mix_ref[0].astype(BF16), wo_ref[0:mix_w, :])
    y = y + _dot(mo_ref[...].astype(BF16), wo_ref[mix_w:, :])
    o_ref[0] = x_ref[0] + y


def _outproj(x, mix, qm, km, vm, gq, wo, tm):
    bsz, length, d = x.shape
    n_mem = km.shape[1]
    mix_w = mix.shape[-1]
    tok = lambda w: pl.BlockSpec((1, tm, w), lambda bi, i: (bi, i, 0))
    mem = pl.BlockSpec((1, n_mem, MEM_W), lambda bi, i: (bi, 0, 0))
    return pl.pallas_call(
        _out_kernel,
        grid=(bsz, length // tm),
        in_specs=[
            tok(d), tok(mix_w), tok(MEM_W), mem, mem,
            pl.BlockSpec((1, HEAD_DIM), lambda bi, i: (0, 0)),
            pl.BlockSpec((d, d), lambda bi, i: (0, 0)),
        ],
        out_specs=tok(d),
        out_shape=jax.ShapeDtypeStruct((bsz, length, d), F32),
        scratch_shapes=[pltpu.VMEM((tm, MEM_W), F32)],
        compiler_params=_params("parallel", "parallel"),
        name="memattn_outproj",
    )(x, mix, qm, km, vm, gq.reshape(1, HEAD_DIM), wo)


def _ffn_kernel(x_ref, g_ref, wg_ref, wu_ref, wd_ref, o_ref, h_ref, acc_ref):
    j = pl.program_id(1)

    @pl.when(j == 0)
    def _():
        x = x_ref[...]
        h_ref[...] = _rms(x, g_ref[...]).astype(BF16)
        acc_ref[...] = x

    h = h_ref[...]
    gate = _dot(h, wg_ref[...])
    up = _dot(h, wu_ref[...])
    a = gate * jax.nn.sigmoid(gate) * up
    acc_ref[...] += _dot(a.astype(BF16), wd_ref[...])

    @pl.when(j == pl.num_programs(1) - 1)
    def _():
        o_ref[...] = acc_ref[...]


def _ffn(x, g, wgu, wdn, tm, tf):
    t, d = x.shape
    d_ff = wdn.shape[0]
    nf = d_ff // tf
    return pl.pallas_call(
        _ffn_kernel,
        grid=(t // tm, nf),
        in_specs=[
            pl.BlockSpec((tm, d), lambda i, j: (i, 0)),
            pl.BlockSpec((1, d), lambda i, j: (0, 0)),
            pl.BlockSpec((d, tf), lambda i, j: (0, j)),
            pl.BlockSpec((d, tf), lambda i, j: (0, nf + j)),
            pl.BlockSpec((tf, d), lambda i, j: (j, 0)),
        ],
        out_specs=pl.BlockSpec((tm, d), lambda i, j: (i, 0)),
        out_shape=jax.ShapeDtypeStruct((t, d), F32),
        scratch_shapes=[pltpu.VMEM((tm, d), BF16), pltpu.VMEM((tm, d), F32)],
        compiler_params=_params("parallel", "arbitrary"),
        name="swiglu_ffn",
    )(x, g.reshape(1, d), wgu, wgu, wdn)


def _inproj_fox_kernel(x_ref, g_ref, w_ref, bd_ref, gq_ref, gk_ref, bf_ref,
                       qb_ref, k_ref, kb_ref, v_ref, vb_ref, lf_ref, qm_ref):
    h = _rms(x_ref[...], g_ref[...])
    z = _dot(h.astype(BF16), w_ref[...])
    q = z[:, :FOX_W]
    k = z[:, FOX_W:2 * FOX_W]
    v = z[:, 2 * FOX_W:3 * FOX_W]
    qn = q * lax.rsqrt(_head_ms(q, bd_ref) + EPS) * gq_ref[...]
    kn = k * lax.rsqrt(_head_ms(k, bd_ref) + EPS) * gk_ref[...]
    qb_ref[...] = (qn * SCALE).astype(BF16)
    k_ref[...] = kn
    kb_ref[...] = kn.astype(BF16)
    v_ref[...] = v
    vb_ref[...] = v.astype(BF16)
    qm_ref[...] = z[:, 3 * FOX_W:3 * FOX_W + MEM_W]
    zf = z[:, 3 * FOX_W + MEM_W:] + bf_ref[...]
    lf_ref[...] = -(jnp.maximum(-zf, 0.0) + jnp.log1p(jnp.exp(-jnp.abs(zf))))


def _inproj_fox(x, g, w, gq, gk, bf, tm):
    t, d = x.shape
    n = w.shape[1]
    row = lambda width: pl.BlockSpec((tm, width), lambda i: (i, 0))
    const = lambda r, c: pl.BlockSpec((r, c), lambda i: (0, 0))
    sds = lambda width, dt: jax.ShapeDtypeStruct((t, width), dt)
    return pl.pallas_call(
        _inproj_fox_kernel,
        grid=(t // tm,),
        in_specs=[row(d), const(1, d), const(d, n), const(FOX_W, FOX_W), const(1, FOX_W), const(1, FOX_W),
                  const(1, LANES)],
        out_specs=[row(FOX_W), row(FOX_W), row(FOX_W), row(FOX_W), row(FOX_W), row(LANES), row(MEM_W)],
        out_shape=[sds(FOX_W, BF16), sds(FOX_W, F32), sds(FOX_W, BF16), sds(FOX_W, F32), sds(FOX_W, BF16),
                   sds(LANES, F32), sds(MEM_W, F32)],
        compiler_params=_params("parallel"),
        name="inproj_fox",
    )(x, g.reshape(1, d), w, _block_diag_ones(FOX_HEADS), jnp.tile(gq, FOX_HEADS).reshape(1, FOX_W),
      jnp.tile(gk, FOX_HEADS).reshape(1, FOX_W), jnp.pad(bf, (0, LANES - FOX_HEADS)).reshape(1, LANES))


def _cumsum_kernel(lf_ref, c_ref, *, ch, seg):
    r = lax.broadcasted_iota(jnp.int32, (ch, ch), 0)
    c = lax.broadcasted_iota(jnp.int32, (ch, ch), 1)
    tri = r >= c
    if seg:
        tri = tri & (r // seg == c // seg)
    tri = tri.astype(BF16)

    def body(i, carry):
        x = lf_ref[0, pl.ds(i * ch, ch), :]
        a, b, c3 = _split3(x)
        cs = _dot(tri, a) + _dot(tri, b) + _dot(tri, c3) + carry
        c_ref[0, pl.ds(i * ch, ch), :] = cs
        return carry if seg else cs[ch - 1:ch, :]

    lax.fori_loop(0, lf_ref.shape[1] // ch, body, jnp.zeros((1, lf_ref.shape[2]), F32))


def _cumsum(lf, ch, seg=0):
    bsz, length, w = lf.shape
    spec = pl.BlockSpec((1, length, w), lambda bi: (bi, 0, 0))
    return pl.pallas_call(
        functools.partial(_cumsum_kernel, ch=ch, seg=seg),
        grid=(bsz,),
        in_specs=[spec],
        out_specs=spec,
        out_shape=jax.ShapeDtypeStruct(lf.shape, F32),
        compiler_params=_params("parallel"),
        name="logf_cumsum",
    )(lf)


def _flash_kernel(q_ref, k_ref, v_ref, c_ref, ct_ref, o_ref, *, tq):
    i = pl.program_id(1)
    row = lax.broadcasted_iota(jnp.int32, (tq, tq), 0)
    col = lax.broadcasted_iota(jnp.int32, (tq, tq), 1)
    causal = row >= col
    for h in range(FOX_HEADS):
        sl = slice(h * HEAD_DIM, (h + 1) * HEAD_DIM)
        qh = q_ref[0, :, sl]
        cq = c_ref[0, :, h:h + 1]

        def tile(j, carry, masked, sl=sl, qh=qh, cq=cq, h=h):
            m, l, acc = carry
            kh = k_ref[0, pl.ds(j * tq, tq), sl]
            vh = v_ref[0, pl.ds(j * tq, tq), sl]
            s = _dot_nt(qh, kh) + cq - ct_ref[0, h, pl.ds(j, 1), :]
            if masked:
                s = jnp.where(causal, s, NEG_INF)
            m_new = jnp.maximum(m, jnp.max(s, axis=-1, keepdims=True))
            alpha = jnp.exp(m - m_new)
            p = jnp.exp(s - m_new)
            l = alpha * l + jnp.sum(p, axis=-1, keepdims=True)
            acc = alpha * acc + _dot(p.astype(BF16), vh)
            return m_new, l, acc

        init = (jnp.full((tq, 1), NEG_INF, F32), jnp.zeros((tq, 1), F32), jnp.zeros((tq, HEAD_DIM), F32))
        carry = lax.fori_loop(0, i, functools.partial(tile, masked=False), init)
        _, l, acc = tile(i, carry, True)
        o_ref[0, :, sl] = acc / l


def _fox_prompt(qb, kb, vb, c, ct, tq):
    bsz, length, w = qb.shape
    nk = length // tq
    seq = pl.BlockSpec((1, length, w), lambda bi, i: (bi, 0, 0))
    return pl.pallas_call(
        functools.partial(_flash_kernel, tq=tq),
        grid=(bsz, nk),
        in_specs=[
            pl.BlockSpec((1, tq, w), lambda bi, i: (bi, i, 0)),
            seq, seq,
            pl.BlockSpec((1, tq, LANES), lambda bi, i: (bi, i, 0)),
            pl.BlockSpec((1, FOX_HEADS, nk, tq), lambda bi, i: (bi, 0, 0, 0)),
        ],
        out_specs=pl.BlockSpec((1, tq, w), lambda bi, i: (bi, i, 0)),
        out_shape=jax.ShapeDtypeStruct((bsz, length, w), F32),
        compiler_params=_params("parallel", "parallel"),
        name="fox_prompt_attention",
    )(qb, kb, vb, c, ct)


def _decode_kernel(pt_ref, q_ref, kn_ref, vn_ref, cqc_ref, cqr_ref, *refs, n_pp, n_pages, page):
    k_refs = refs[:n_pp]
    v_refs = refs[n_pp:2 * n_pp]
    lt_refs = refs[2 * n_pp:3 * n_pp]
    o_ref, qbd_ref, m_ref, l_ref, acc_ref, tot_ref = refs[3 * n_pp:]
    b = pl.program_id(0)
    g = pl.program_id(1)
    td = q_ref.shape[1]
    rows = FOX_HEADS * td
    head_of_row = lax.broadcasted_iota(jnp.int32, (rows, FOX_W), 0) // td
    head_of_col = lax.broadcasted_iota(jnp.int32, (rows, FOX_W), 1) // HEAD_DIM
    own_head = head_of_row == head_of_col

    @pl.when(g == 0)
    def _():
        q_rep = jnp.broadcast_to(q_ref[0][None], (FOX_HEADS, td, FOX_W)).reshape(rows, FOX_W)
        qbd = jnp.where(own_head, q_rep, 0.0).astype(BF16)
        qbd_ref[...] = qbd
        s = _dot_nt(qbd, kn_ref[0].astype(BF16)) + cqc_ref[0] - cqr_ref[0]
        t_of_row = lax.broadcasted_iota(jnp.int32, (rows, page), 0) % td
        t_of_col = lax.broadcasted_iota(jnp.int32, (rows, page), 1)
        s = jnp.where(t_of_col <= t_of_row, s, NEG_INF)
        m = jnp.max(s, axis=-1, keepdims=True)
        p = jnp.exp(s - m)
        m_ref[...] = m
        l_ref[...] = jnp.sum(p, axis=-1, keepdims=True)
        acc_ref[...] = _dot(p.astype(BF16), vn_ref[0].astype(BF16))
        tot_ref[...] = jnp.zeros_like(tot_ref)

    r_i = lax.broadcasted_iota(jnp.int32, (page, page), 0)
    c_i = lax.broadcasted_iota(jnp.int32, (page, page), 1)
    later = (r_i > c_i).astype(BF16)
    qbd = qbd_ref[...]
    cq = cqc_ref[0]
    for pi in range(n_pp):
        pg = pt_ref[b, n_pages - 1 - (g * n_pp + pi)]
        lf = lt_refs[pi][:, pl.ds(pg % 8, 1), :]
        lf = jnp.broadcast_to(lf, (FOX_HEADS, td, page)).reshape(rows, page)
        a3, b3, c3 = _split3(lf)
        tot = tot_ref[...]
        decay = _dot(a3, later) + _dot(b3, later) + _dot(c3, later) + tot
        tot_ref[...] = tot + jnp.sum(lf, axis=-1, keepdims=True)
        s = _dot(qbd, k_refs[pi][0].astype(BF16)) + cq + decay
        m = m_ref[...]
        m_new = jnp.maximum(m, jnp.max(s, axis=-1, keepdims=True))
        alpha = jnp.exp(m - m_new)
        p = jnp.exp(s - m_new)
        m_ref[...] = m_new
        l_ref[...] = alpha * l_ref[...] + jnp.sum(p, axis=-1, keepdims=True)
        acc_ref[...] = alpha * acc_ref[...] + _dot_nt(p.astype(BF16), v_refs[pi][0].astype(BF16))

    @pl.when(g == pl.num_programs(1) - 1)
    def _():
        out = jnp.where(own_head, acc_ref[...] / l_ref[...], 0.0)
        o_ref[0] = jnp.sum(out.reshape(FOX_HEADS, td, FOX_W), axis=0)


def _fox_sample(page_table, q, k_new, v_new, cq_col, cq_row, k_pages, v_pages, lf_pages, n_pp):
    bd, td, w = q.shape
    n_pages = page_table.shape[1]
    page = k_pages.shape[2]
    rows = FOX_HEADS * td
    per_b = lambda r, c: pl.BlockSpec((1, r, c), lambda b, g, pt: (b, 0, 0))

    def kv_spec(pi):
        return pl.BlockSpec((1, w, page), lambda b, g, pt: (pt[b, n_pages - 1 - (g * n_pp + pi)], 0, 0))

    def lf_spec(pi):
        return pl.BlockSpec((FOX_HEADS, 8, page), lambda b, g, pt: (0, pt[b, n_pages - 1 - (g * n_pp + pi)] // 8, 0))

    grid_spec = pltpu.PrefetchScalarGridSpec(
        num_scalar_prefetch=1,
        grid=(bd, n_pages // n_pp),
        in_specs=[per_b(td, w), per_b(page, w), per_b(page, w), per_b(rows, 1), per_b(rows, page)]
        + [kv_spec(pi) for pi in range(n_pp)] + [kv_spec(pi) for pi in range(n_pp)]
        + [lf_spec(pi) for pi in range(n_pp)],
        out_specs=per_b(td, w),
        scratch_shapes=[pltpu.VMEM((rows, w), BF16), pltpu.VMEM((rows, 1), F32), pltpu.VMEM((rows, 1), F32),
                        pltpu.VMEM((rows, w), F32), pltpu.VMEM((rows, 1), F32)],
    )
    return pl.pallas_call(
        functools.partial(_decode_kernel, n_pp=n_pp, n_pages=n_pages, page=page),
        grid_spec=grid_spec,
        out_shape=jax.ShapeDtypeStruct((bd, td, w), F32),
        compiler_params=_params("parallel", "arbitrary"),
        name="fox_sample_attention",
    )(page_table, q, k_new, v_new, cq_col, cq_row, *([k_pages] * n_pp), *([v_pages] * n_pp), *([lf_pages] * n_pp))


def kernel(x_prompt, x_sample, cache_conv, cache_fox_k, cache_fox_v, cache_fox_logf, cache_mem_k, cache_mem_v, page_table, mem_prompt, norm1, conv_w_in, conv_dw, conv_dw_b, conv_ln_g, conv_ln_b, fox_w_in, fox_b_f, fox_q_norm, fox_k_norm, mem_norm, w_mem_kv, mem_q_norm, mem_k_norm, w_out, norm2, w_gate_up, w_down):
    bsz, seq, d = x_prompt.shape
    bd, td, _ = x_sample.shape
    n_mem = mem_prompt.shape[1]
    n_phys, page = cache_fox_k.shape[1], cache_fox_k.shape[2]
    tm_p, tm_s = 512, bd * td
    mem_tok = mem_prompt.reshape(bsz * n_mem, d)

    def finish(i, xp, xs, mix_p, mix_s, qm_p, qm_s):
        km, vm = _memkv(mem_tok, mem_norm[i], w_mem_kv[i].astype(BF16), mem_k_norm[i], tm=512)
        wo = w_out[i].astype(BF16)
        xp = _outproj(xp.reshape(bsz, seq, d), mix_p.reshape(bsz, seq, -1), qm_p.reshape(bsz, seq, MEM_W),
                      km.reshape(bsz, n_mem, MEM_W).astype(BF16), vm.reshape(bsz, n_mem, MEM_W).astype(BF16),
                      mem_q_norm[i], wo, tm=tm_p)
        xs = _outproj(xs.reshape(bd, td, d), mix_s.reshape(bd, td, -1), qm_s.reshape(bd, td, MEM_W),
                      cache_mem_k[i].reshape(bd, n_mem, MEM_W).astype(BF16),
                      cache_mem_v[i].reshape(bd, n_mem, MEM_W).astype(BF16), mem_q_norm[i], wo, tm=td)
        wgu = w_gate_up[i].astype(BF16)
        wdn = w_down[i].astype(BF16)
        xp = _ffn(xp.reshape(bsz * seq, d), norm2[i], wgu, wdn, tm=1024, tf=256)
        xs = _ffn(xs.reshape(bd * td, d), norm2[i], wgu, wdn, tm=tm_s, tf=256)
        return xp, xs, km, vm

    xp = x_prompt.reshape(bsz * seq, d)
    xs = x_sample.reshape(bd * td, d)

    w0 = conv_w_in[0].astype(BF16)
    u_p, qm_p = _inproj_conv(xp, norm1[0], w0, tm=tm_p)
    u_s, qm_s = _inproj_conv(xs, norm1[0], w0, tm=tm_s)
    u_p = u_p.reshape(bsz, seq, CONV_CH)
    u_s = u_s.reshape(bd, td, CONV_CH)
    conv_args = (conv_dw[0], conv_dw_b[0], conv_ln_g[0], conv_ln_b[0])
    mix_p = _conv(u_p, None, *conv_args, tt=256, rc=32)
    hist_s = jnp.pad(cache_conv[0], ((0, 0), (CONV_HIST - (CONV_WIDTH - 1), 0), (0, 0)))
    mix_s = _conv(u_s, hist_s, *conv_args, tt=td, rc=td)
    new_conv_p = u_p[:, seq - (CONV_WIDTH - 1):][None]
    new_conv_s = jnp.concatenate([cache_conv[0], u_s], axis=1)[:, td:][None]
    xp, xs, km0, vm0 = finish(0, xp, xs, mix_p, mix_s, qm_p, qm_s)

    wf = fox_w_in[0]
    n_qkv = 3 * FOX_W
    w1 = jnp.concatenate([wf[:, :n_qkv], wf[:, n_qkv + FOX_HEADS:], wf[:, n_qkv:n_qkv + FOX_HEADS],
                          jnp.zeros((d, LANES - FOX_HEADS), F32)], axis=1).astype(BF16)
    fox_args = (norm1[1], w1, fox_q_norm[0], fox_k_norm[0], fox_b_f[0])
    qb_p, k_p, kb_p, v_p, vb_p, lf_p, qm_p = _inproj_fox(xp, *fox_args, tm=tm_p)
    qb_s, k_s, _, v_s, _, lf_s, qm_s = _inproj_fox(xs, *fox_args, tm=tm_s)

    tq = 512
    c_p = _cumsum(lf_p.reshape(bsz, seq, LANES), ch=256)
    ct_p = jnp.transpose(c_p[:, :, :FOX_HEADS], (0, 2, 1)).reshape(bsz, FOX_HEADS, seq // tq, tq)
    seq3 = lambda a: a.reshape(bsz, seq, FOX_W)
    mix_p = _fox_prompt(seq3(qb_p), seq3(kb_p), seq3(vb_p), c_p, ct_p, tq=tq)

    c_s = _cumsum(lf_s.reshape(1, bd * td, LANES), ch=bd * td, seg=td)
    c_s = c_s.reshape(bd, td, LANES)[:, :, :FOX_HEADS]
    c_ht = jnp.transpose(c_s, (0, 2, 1))
    cq_col = c_ht.reshape(bd, FOX_HEADS * td, 1)
    cq_row = jnp.broadcast_to(c_ht[:, :, None, :], (bd, FOX_HEADS, td, td)).reshape(bd, FOX_HEADS * td, td)
    cq_row = jnp.pad(cq_row, ((0, 0), (0, 0), (0, page - td)))
    pad_t = lambda a: jnp.pad(a.reshape(bd, td, FOX_W), ((0, 0), (0, page - td), (0, 0)))
    k_pages = jnp.transpose(cache_fox_k[0], (0, 2, 3, 1)).reshape(n_phys, FOX_W, page)
    v_pages = jnp.transpose(cache_fox_v[0], (0, 2, 3, 1)).reshape(n_phys, FOX_W, page)
    lf_pages = jnp.transpose(cache_fox_logf[0], (2, 0, 1))
    mix_s = _fox_sample(page_table, qb_s.astype(F32).reshape(bd, td, FOX_W), pad_t(k_s), pad_t(v_s), cq_col, cq_row,
                        k_pages, v_pages, lf_pages, n_pp=8)
    xp, xs, km1, vm1 = finish(1, xp, xs, mix_p, mix_s, qm_p, qm_s)

    heads = lambda a, b_, l_: a.reshape(1, b_, l_, FOX_HEADS, HEAD_DIM)
    mem_heads = lambda a0, a1: jnp.stack([a0, a1]).reshape(2, bsz, n_mem, MEM_HEADS, HEAD_DIM)
    return (xp.reshape(bsz, seq, d), xs.reshape(bd, td, d), new_conv_p,
            heads(k_p, bsz, seq), heads(v_p, bsz, seq), lf_p[:, :FOX_HEADS].reshape(1, bsz, seq, FOX_HEADS),
            mem_heads(km0, km1), mem_heads(vm0, vm1), new_conv_s,
            heads(k_s, bd, td), heads(v_s, bd, td), lf_s[:, :FOX_HEADS].reshape(1, bd, td, FOX_HEADS))
```
